```python
import math
import jax, jax.numpy as jnp
from jax import lax
import numpy as np

D_MODEL = 2048
BATCH = 8
SEQ = 2048
DEPTH = 1

N_MEM = 256
MEM_HEADS = 4
MEM_HEAD_DIM = 128
MEM_WIDTH = MEM_HEADS * MEM_HEAD_DIM

MLA_HEADS = 8
QK_NOPE = 128
QK_ROPE = 64
V_HEAD = 128
Q_LORA = 512
KV_LORA = 512
ROPE_THETA = 10000.0
MAX_POS_OFFSET = 4096
Q_BLOCK = 128
MLA_WIDTH = MLA_HEADS * V_HEAD

LRU_WIDTH = 1024
LRU_BLOCKS = 8
LRU_BLOCK_DIM = LRU_WIDTH // LRU_BLOCKS
CONV_WIDTH = 4
CONV_PAD_LEFT = 2
LRU_C = 8.0

MIX_WIDTH = MLA_WIDTH + LRU_WIDTH
SPLIT_CQ = Q_LORA
SPLIT_CKV = SPLIT_CQ + KV_LORA
SPLIT_KPE = SPLIT_CKV + QK_ROPE
SPLIT_LRUX = SPLIT_KPE + LRU_WIDTH
IN_WIDTH = SPLIT_LRUX + LRU_WIDTH

PEER_HEADS = 8
N_KEYS = 128
N_EXPERTS = N_KEYS * N_KEYS
PEER_QUERY = 256
PEER_HALF = PEER_QUERY // 2
PEER_TOPK = 16
TOKEN_BLOCK = 128

DN_ALPHA = (2.0 * DEPTH) ** 0.25
DN_BETA = (8.0 * DEPTH) ** -0.25
LN_EPS = 1e-5
RMS_EPS = 1e-6

kernel_name = "hybrid_mla_rglru_peer_encoder"


def layer_norm(x, g, b):
    xf = x.astype(jnp.float32)
    mu = jnp.mean(xf, axis=-1, keepdims=True)
    var = jnp.mean(jnp.square(xf - mu), axis=-1, keepdims=True)
    return ((xf - mu) * lax.rsqrt(var + LN_EPS)).astype(x.dtype) * g + b


def rms_norm(x, g):
    xf = x.astype(jnp.float32)
    return (xf * lax.rsqrt(jnp.mean(xf * xf, axis=-1, keepdims=True) + RMS_EPS)).astype(x.dtype) * g


def rope_tables(positions, dtype):
    inv_freq = ROPE_THETA ** (-jnp.arange(0, QK_ROPE, 2, dtype=jnp.float32) / QK_ROPE)
    ang = positions.astype(jnp.float32)[..., None] * inv_freq
    return jnp.cos(ang).astype(dtype), jnp.sin(ang).astype(dtype)


def apply_rope(t, cos, sin):
    t1, t2 = jnp.split(t, 2, axis=-1)
    return jnp.concatenate([t1 * cos - t2 * sin, t2 * cos + t1 * sin], axis=-1)


def mla_attention(c_q, c_kv, k_pe, cos, sin, q_norm_g, w_uq, kv_norm_g, w_ukv):
    B, S, _ = c_q.shape
    q = (rms_norm(c_q, q_norm_g) @ w_uq).reshape(B, S, MLA_HEADS, QK_NOPE + QK_ROPE)
    q_nope = q[..., :QK_NOPE]
    q_pe = apply_rope(q[..., QK_NOPE:], cos[:, :, None, :], sin[:, :, None, :])
    kv = (rms_norm(c_kv, kv_norm_g) @ w_ukv).reshape(B, S, MLA_HEADS, QK_NOPE + V_HEAD)
    k_nope = kv[..., :QK_NOPE]
    v = kv[..., QK_NOPE:]
    k_rot = apply_rope(k_pe, cos, sin)
    scale = (QK_NOPE + QK_ROPE) ** -0.5
    nb = S // Q_BLOCK

    def to_blocks(t):
        return t.reshape(B, nb, Q_BLOCK, *t.shape[2:]).swapaxes(0, 1)

    def attend(blk):
        qn, qp = blk
        s = (jnp.einsum('bqhd,bkhd->bhqk', qn, k_nope)
             + jnp.einsum('bqhr,bkr->bhqk', qp, k_rot))
        p = jax.nn.softmax(s.astype(jnp.float32) * scale, axis=-1).astype(v.dtype)
        return jnp.einsum('bhqk,bkhd->bqhd', p, v)

    o = lax.map(attend, (to_blocks(q_nope), to_blocks(q_pe)))
    return o.swapaxes(0, 1).reshape(B, S, MLA_WIDTH)


def centred_depthwise_conv(x, w, b):
    S = x.shape[1]
    xp = jnp.pad(x, ((0, 0), (CONV_PAD_LEFT, CONV_WIDTH - 1 - CONV_PAD_LEFT), (0, 0)))
    out = b + xp[:, 0:S] * w[0]
    for k in range(1, CONV_WIDTH):
        out = out + xp[:, k:k + S] * w[k]
    return out


def rg_lru(xc, wa, ba, wx, bx, lam, reverse):
    B, S, _ = xc.shape
    xb = xc.reshape(B, S, LRU_BLOCKS, LRU_BLOCK_DIM)
    r = jax.nn.sigmoid(jnp.einsum('bsni,nij->bsnj', xb, wa).reshape(B, S, LRU_WIDTH) + ba)
    i = jax.nn.sigmoid(jnp.einsum('bsni,nij->bsnj', xb, wx).reshape(B, S, LRU_WIDTH) + bx)
    log_a = -LRU_C * r.astype(jnp.float32) * jax.nn.softplus(-lam.astype(jnp.float32))
    a = jnp.exp(log_a)
    b_in = jnp.sqrt(-jnp.expm1(2.0 * log_a)) * (i * xc).astype(jnp.float32)

    def combine(e1, e2):
        a1, b1 = e1
        a2, b2 = e2
        return a1 * a2, a2 * b1 + b2

    _, h = lax.associative_scan(combine, (a, b_in), axis=1, reverse=reverse)
    return h.astype(xc.dtype)


def memory_cross_attention(x, mem, wq, wk, wv, wo):
    B, S, _ = x.shape
    M = mem.shape[1]
    q = (x @ wq).reshape(B, S, MEM_HEADS, MEM_HEAD_DIM)
    k = (mem @ wk).reshape(B, M, MEM_HEADS, MEM_HEAD_DIM)
    v = (mem @ wv).reshape(B, M, MEM_HEADS, MEM_HEAD_DIM)
    s = jnp.einsum('bshd,bmhd->bhsm', q, k).astype(jnp.float32) * (MEM_HEAD_DIM ** -0.5)
    p = jax.nn.softmax(s, axis=-1).astype(v.dtype)
    o = jnp.einsum('bhsm,bmhd->bshd', p, v).reshape(B, S, MEM_WIDTH)
    return o @ wo


def peer_ffn(x, w_pq, sub_keys1, sub_keys2, peer_u, peer_v):
    B, S, D = x.shape
    T = B * S
    xt = x.reshape(T, D)
    q = (xt @ w_pq).reshape(T, PEER_HEADS, PEER_QUERY)
    s1 = jnp.einsum('thd,nd->thn', q[..., :PEER_HALF], sub_keys1).astype(jnp.float32)
    s2 = jnp.einsum('thd,nd->thn', q[..., PEER_HALF:], sub_keys2).astype(jnp.float32)
    v1, i1 = lax.top_k(s1, PEER_TOPK)
    v2, i2 = lax.top_k(s2, PEER_TOPK)
    cand_s = (v1[..., :, None] + v2[..., None, :]).reshape(T, PEER_HEADS, PEER_TOPK * PEER_TOPK)
    cand_i = (i1[..., :, None] * N_KEYS + i2[..., None, :]).reshape(T, PEER_HEADS, PEER_TOPK * PEER_TOPK)
    top_s, pos = lax.top_k(cand_s, PEER_TOPK)
    idx = jnp.take_along_axis(cand_i, pos, axis=-1)
    gate = jax.nn.softmax(top_s, axis=-1).astype(x.dtype)
    nb = T // TOKEN_BLOCK

    def experts(blk):
        xb, ib, gb = blk
        u_sel = jnp.take(peer_u, ib, axis=0)
        act = jax.nn.gelu(jnp.einsum('thkd,td->thk', u_sel, xb), approximate=False) * gb
        v_sel = jnp.take(peer_v, ib, axis=0)
        return jnp.einsum('thk,thkd->td', act, v_sel)

    out = lax.map(experts, (xt.reshape(nb, TOKEN_BLOCK, D),
                            idx.reshape(nb, TOKEN_BLOCK, PEER_HEADS, PEER_TOPK),
                            gate.reshape(nb, TOKEN_BLOCK, PEER_HEADS, PEER_TOPK)))
    return out.reshape(B, S, D)


def setup_inputs(seed: int = 0) -> dict:
    key = jax.random.key(seed)
    ks = iter(jax.random.split(key, 48))

    def nrm(shape, scale):
        return jax.random.normal(next(ks), shape, jnp.float32) * scale

    def gain(shape):
        return 1.0 + nrm(shape, 0.02)

    L = DEPTH
    x = nrm((BATCH, SEQ, D_MODEL), 1.0)
    mem = nrm((BATCH, N_MEM, D_MODEL), 1.0)
    offsets = jax.random.randint(next(ks), (BATCH, 1), 0, MAX_POS_OFFSET, dtype=jnp.int32)
    positions = offsets + jnp.arange(SEQ, dtype=jnp.int32)[None, :]
    in_ln_g = gain((D_MODEL,))
    in_ln_b = nrm((D_MODEL,), 0.02)
    w_in = nrm((L, D_MODEL, IN_WIDTH), D_MODEL ** -0.5)
    q_norm_g = gain((L, Q_LORA))
    w_uq = nrm((L, Q_LORA, MLA_HEADS * (QK_NOPE + QK_ROPE)), Q_LORA ** -0.5)
    kv_norm_g = gain((L, KV_LORA))
    w_ukv = nrm((L, KV_LORA, MLA_HEADS * (QK_NOPE + V_HEAD)), KV_LORA ** -0.5)
    conv_w = nrm((L, CONV_WIDTH, LRU_WIDTH), CONV_WIDTH ** -0.5)
    conv_b = nrm((L, LRU_WIDTH), 0.02)
    lru_wa = nrm((L, 2, LRU_BLOCKS, LRU_BLOCK_DIM, LRU_BLOCK_DIM), LRU_BLOCK_DIM ** -0.5)
    lru_ba = nrm((L, 2, LRU_WIDTH), 0.02)
    lru_wx = nrm((L, 2, LRU_BLOCKS, LRU_BLOCK_DIM, LRU_BLOCK_DIM), LRU_BLOCK_DIM ** -0.5)
    lru_bx = nrm((L, 2, LRU_WIDTH), 0.02)
    a0 = jax.random.uniform(next(ks), (L, 2, LRU_WIDTH), jnp.float32, 0.9, 0.999)
    s0 = a0 ** (1.0 / LRU_C)
    lru_lambda = jnp.log(s0) - jnp.log1p(-s0)
    attn_out_g = gain((L, MLA_WIDTH))
    lru_out_g = gain((L, LRU_WIDTH))
    w_out = nrm((L, MIX_WIDTH, D_MODEL), DN_BETA * MIX_WIDTH ** -0.5)
    ln1_g = gain((L, D_MODEL))
    ln1_b = nrm((L, D_MODEL), 0.02)
    w_mq = nrm((L, D_MODEL, MEM_WIDTH), D_MODEL ** -0.5)
    w_mk = nrm((L, D_MODEL, MEM_WIDTH), D_MODEL ** -0.5)
    w_mv = nrm((L, D_MODEL, MEM_WIDTH), D_MODEL ** -0.5)
    w_mo = nrm((L, MEM_WIDTH, D_MODEL), DN_BETA * MEM_WIDTH ** -0.5)
    ln2_g = gain((L, D_MODEL))
    ln2_b = nrm((L, D_MODEL), 0.02)
    w_pq = nrm((L, D_MODEL, PEER_HEADS * PEER_QUERY), D_MODEL ** -0.5)
    sub_keys1 = nrm((L, N_KEYS, PEER_HALF), PEER_HALF ** -0.5)
    sub_keys2 = nrm((L, N_KEYS, PEER_HALF), PEER_HALF ** -0.5)
    peer_u = nrm((L, N_EXPERTS, D_MODEL), D_MODEL ** -0.5)
    peer_v = nrm((L, N_EXPERTS, D_MODEL), DN_BETA)
    ln3_g = gain((L, D_MODEL))
    ln3_b = nrm((L, D_MODEL), 0.02)
    return {
        "x": x, "mem": mem, "positions": positions,
        "in_ln_g": in_ln_g, "in_ln_b": in_ln_b,
        "w_in": w_in, "q_norm_g": q_norm_g, "w_uq": w_uq, "kv_norm_g": kv_norm_g, "w_ukv": w_ukv,
        "conv_w": conv_w, "conv_b": conv_b,
        "lru_wa": lru_wa, "lru_ba": lru_ba, "lru_wx": lru_wx, "lru_bx": lru_bx, "lru_lambda": lru_lambda,
        "attn_out_g": attn_out_g, "lru_out_g": lru_out_g, "w_out": w_out,
        "ln1_g": ln1_g, "ln1_b": ln1_b,
        "w_mq": w_mq, "w_mk": w_mk, "w_mv": w_mv, "w_mo": w_mo,
        "ln2_g": ln2_g, "ln2_b": ln2_b,
        "w_pq": w_pq, "sub_keys1": sub_keys1, "sub_keys2": sub_keys2,
        "peer_u": peer_u, "peer_v": peer_v,
        "ln3_g": ln3_g, "ln3_b": ln3_b,
    }


def reference(x, mem, positions, in_ln_g, in_ln_b, w_in, q_norm_g, w_uq, kv_norm_g, w_ukv,
              conv_w, conv_b, lru_wa, lru_ba, lru_wx, lru_bx, lru_lambda,
              attn_out_g, lru_out_g, w_out, ln1_g, ln1_b,
              w_mq, w_mk, w_mv, w_mo, ln2_g, ln2_b,
              w_pq, sub_keys1, sub_keys2, peer_u, peer_v, ln3_g, ln3_b):
    cos, sin = rope_tables(positions, x.dtype)
    x = layer_norm(x, in_ln_g, in_ln_b)
    for l in range(DEPTH):
        h = x @ w_in[l]
        c_q = h[..., :SPLIT_CQ]
        c_kv = h[..., SPLIT_CQ:SPLIT_CKV]
        k_pe = h[..., SPLIT_CKV:SPLIT_KPE]
        lru_x = h[..., SPLIT_KPE:SPLIT_LRUX]
        lru_gate = h[..., SPLIT_LRUX:]
        y_attn = mla_attention(c_q, c_kv, k_pe, cos, sin, q_norm_g[l], w_uq[l], kv_norm_g[l], w_ukv[l])
        xc = centred_depthwise_conv(lru_x, conv_w[l], conv_b[l])
        h_fwd = rg_lru(xc, lru_wa[l, 0], lru_ba[l, 0], lru_wx[l, 0], lru_bx[l, 0], lru_lambda[l, 0], False)
        h_bwd = rg_lru(xc, lru_wa[l, 1], lru_ba[l, 1], lru_wx[l, 1], lru_bx[l, 1], lru_lambda[l, 1], True)
        y_lru = (h_fwd + h_bwd) * jax.nn.gelu(lru_gate)
        y_mix = jnp.concatenate([rms_norm(y_attn, attn_out_g[l]), rms_norm(y_lru, lru_out_g[l])], axis=-1)
        x = layer_norm(DN_ALPHA * x + y_mix @ w_out[l], ln1_g[l], ln1_b[l])
        y_mem = memory_cross_attention(x, mem, w_mq[l], w_mk[l], w_mv[l], w_mo[l])
        x = layer_norm(DN_ALPHA * x + y_mem, ln2_g[l], ln2_b[l])
        y_ffn = peer_ffn(x, w_pq[l], sub_keys1[l], sub_keys2[l], peer_u[l], peer_v[l])
        x = layer_norm(DN_ALPHA * x + y_ffn, ln3_g[l], ln3_b[l])
    return x
```

```python
import functools
import math

import jax
import jax.numpy as jnp
from jax import lax
from jax.experimental import pallas as pl
from jax.experimental.pallas import tpu as pltpu

MLA_HEADS = 8
QK_NOPE = 128
QK_ROPE = 64
V_HEAD = 128
Q_LORA = 512
KV_LORA = 512
ROPE_THETA = 10000.0
LRU_WIDTH = 1024
LRU_BLOCKS = 8
LRU_BLOCK_DIM = 128
CONV_WIDTH = 4
LRU_C = 8.0
MEM_HEADS = 4
MEM_HEAD_DIM = 128
PEER_HEADS = 8
N_KEYS = 128
PEER_HALF = 128
PEER_TOPK = 16
DEPTH = 1
DN_ALPHA = (2.0 * DEPTH) ** 0.25
LN_EPS = 1e-5
RMS_EPS = 1e-6

LANES = 128
SUBLANES = 8
HEAD_PAD = 256
VMEM_LIMIT = 56 * 1024 * 1024

F32 = jnp.float32
BF16 = jnp.bfloat16
NEG_INF = float("-inf")


def _cparams(sem):
    return pltpu.CompilerParams(dimension_semantics=sem, vmem_limit_bytes=VMEM_LIMIT)


def _const_spec(shape):
    nd = len(shape)
    return pl.BlockSpec(shape, lambda *_: (0,) * nd, pipeline_mode=pl.Buffered(1))


def _dot(a, b):
    return jnp.dot(a, b, preferred_element_type=F32)


def _dot_nt(a, b):
    return lax.dot_general(a, b, (((1,), (1,)), ((), ())), preferred_element_type=F32)


def _dot_tn(a, b):
    return lax.dot_general(a, b, (((0,), (0,)), ((), ())), preferred_element_type=F32)


def _layer_norm(x, g, b):
    mu = jnp.mean(x, axis=-1, keepdims=True)
    xc = x - mu
    var = jnp.mean(xc * xc, axis=-1, keepdims=True)
    return xc * lax.rsqrt(var + LN_EPS) * g + b


def _rms_norm(x, g):
    return x * lax.rsqrt(jnp.mean(x * x, axis=-1, keepdims=True) + RMS_EPS) * g


def _gelu_tanh(x):
    c = math.sqrt(2.0 / math.pi)
    return 0.5 * x * (1.0 + jnp.tanh(c * (x + 0.044715 * (x * x * x))))


def _gelu_erf(x):
    return 0.5 * x * (1.0 + lax.erf(x * (1.0 / math.sqrt(2.0))))


def _in_proj_kernel(x_ref, pos_ref, g_ref, b_ref, win_ref, qg_ref, wuq_ref, kvg_ref, wukv_ref,
                    invf_ref, cmask_ref, smask_ref,
                    xn_ref, q_ref, k_ref, v_ref, lrux_ref, glu_ref):
    xn = _layer_norm(x_ref[...], g_ref[...], b_ref[...])
    xn_ref[...] = xn
    h = _dot(xn.astype(BF16), win_ref[...])
    c_q = h[:, 0:Q_LORA]
    c_kv = h[:, Q_LORA:Q_LORA + KV_LORA]
    kpe = h[:, 1024:1152]
    lrux_ref[...] = h[:, 1152:1152 + LRU_WIDTH]
    glu_ref[...] = _gelu_tanh(h[:, 1152 + LRU_WIDTH:])

    ang = pos_ref[...].astype(F32) * invf_ref[...]
    cos_t = jnp.cos(ang) * cmask_ref[...]
    sin_t = jnp.sin(ang) * smask_ref[...]

    def rope(grp):
        return grp * cos_t + pltpu.roll(grp, 64, 1) * sin_t

    q = _dot(_rms_norm(c_q, qg_ref[...]).astype(BF16), wuq_ref[...])
    kv = _dot(_rms_norm(c_kv, kvg_ref[...]).astype(BF16), wukv_ref[...])
    k_rot = rope(kpe).astype(BF16)
    for hd in range(MLA_HEADS):
        base = hd * HEAD_PAD
        q_ref[:, base:base + QK_NOPE] = q[:, base:base + QK_NOPE].astype(BF16)
        q_ref[:, base + QK_NOPE:base + HEAD_PAD] = rope(q[:, base + QK_NOPE:base + HEAD_PAD]).astype(BF16)
        k_ref[:, base:base + QK_NOPE] = kv[:, hd * QK_NOPE:(hd + 1) * QK_NOPE].astype(BF16)
        k_ref[:, base + QK_NOPE:base + HEAD_PAD] = k_rot
    v_ref[...] = kv[:, MLA_HEADS * QK_NOPE:].astype(BF16)


def _in_proj(x2, pos2, in_ln_g, in_ln_b, w_in_p, q_norm_g, w_uq_p, kv_norm_g, w_ukv_p, B, S, tm):
    T, D = x2.shape
    nt = S // tm
    invf = ROPE_THETA ** (-jnp.arange(0, QK_ROPE, 2, dtype=F32) / QK_ROPE)
    invf4 = jnp.tile(invf, 4)[None, :]
    ones, zeros = jnp.ones((32,), F32), jnp.zeros((32,), F32)
    cmask = jnp.concatenate([ones, ones, zeros, zeros])[None, :]
    smask = jnp.concatenate([-ones, ones, zeros, zeros])[None, :]
    tok = lambda w: pl.BlockSpec((tm, w), lambda b, i: (b * nt + i, 0))
    in_w = w_in_p.shape[1]
    return pl.pallas_call(
        _in_proj_kernel,
        grid=(B, nt),
        in_specs=[
            tok(D), tok(1),
            _const_spec((1, D)), _const_spec((1, D)), _const_spec((D, in_w)),
            _const_spec((1, Q_LORA)), _const_spec((Q_LORA, MLA_HEADS * HEAD_PAD)),
            _const_spec((1, KV_LORA)), _const_spec((KV_LORA, 2 * MLA_HEADS * QK_NOPE)),
            _const_spec((1, LANES)), _const_spec((1, LANES)), _const_spec((1, LANES)),
        ],
        out_specs=[
            tok(D), tok(MLA_HEADS * HEAD_PAD), tok(MLA_HEADS * HEAD_PAD), tok(MLA_HEADS * V_HEAD),
            pl.BlockSpec((tm, LRU_WIDTH), lambda b, i: (i, b)),
            tok(LRU_WIDTH),
        ],
        out_shape=[
            jax.ShapeDtypeStruct((T, D), F32),
            jax.ShapeDtypeStruct((T, MLA_HEADS * HEAD_PAD), BF16),
            jax.ShapeDtypeStruct((T, MLA_HEADS * HEAD_PAD), BF16),
            jax.ShapeDtypeStruct((T, MLA_HEADS * V_HEAD), BF16),
            jax.ShapeDtypeStruct((S, B * LRU_WIDTH), F32),
            jax.ShapeDtypeStruct((T, LRU_WIDTH), F32),
        ],
        compiler_params=_cparams(("parallel", "parallel")),
        name="in_proj",
    )(x2, pos2, in_ln_g[None, :], in_ln_b[None, :], w_in_p, q_norm_g[None, :], w_uq_p,
      kv_norm_g[None, :], w_ukv_p, invf4, cmask, smask)


def _mla_attn_kernel(q_ref, k_ref, v_ref, o_ref):
    scale = (QK_NOPE + QK_ROPE) ** -0.5
    s = _dot_nt(q_ref[...], k_ref[...]) * scale
    m = jnp.max(s, axis=-1, keepdims=True)
    p = jnp.exp(s - m)
    l = jnp.sum(p, axis=-1, keepdims=True)
    o = _dot(p.astype(BF16), v_ref[...])
    o_ref[...] = o / l


def _mla_attn(qcat, kcat, v, B, S, tq):
    T = qcat.shape[0]
    nq = S // tq
    return pl.pallas_call(
        _mla_attn_kernel,
        grid=(B, MLA_HEADS, nq),
        in_specs=[
            pl.BlockSpec((tq, HEAD_PAD), lambda b, h, i: (b * nq + i, h)),
            pl.BlockSpec((S, HEAD_PAD), lambda b, h, i: (b, h)),
            pl.BlockSpec((S, V_HEAD), lambda b, h, i: (b, h)),
        ],
        out_specs=pl.BlockSpec((tq, V_HEAD), lambda b, h, i: (b * nq + i, h)),
        out_shape=jax.ShapeDtypeStruct((T, MLA_HEADS * V_HEAD), F32),
        compiler_params=_cparams(("parallel", "parallel", "arbitrary")),
        name="mla_attn",
    )(qcat, kcat, v)


def _rglru_kernel(xf_ref, xfp_ref, xfn_ref, xb_ref, xbp_ref, xbn_ref,
                  cw_ref, cb_ref, wa_ref, ba_ref, wx_ref, bx_ref, lam_ref,
                  hf_ref, hb_ref,
                  xp_scr, a_scr, b_scr, hf_state, hb_state, *, ct, nc):
    j = pl.program_id(1)
    nb = xf_ref.shape[1]

    @pl.when(j == 0)
    def _():
        hf_state[...] = jnp.zeros_like(hf_state)
        hb_state[...] = jnp.zeros_like(hb_state)

    def gates(x_ref, xprev_ref, xnext_ref, chunk, d):
        xp_scr[0:2] = jnp.where(chunk > 0, xprev_ref[...], 0.0)
        xp_scr[2:ct + 2] = x_ref[...]
        xp_scr[ct + 2:ct + 3] = jnp.where(chunk < nc - 1, xnext_ref[...], 0.0)
        xc = cb_ref[...][None]
        for kk in range(CONV_WIDTH):
            xc = xc + xp_scr[kk:kk + ct] * cw_ref[kk:kk + 1, :][None]
        xc2 = xc.reshape(ct * nb, LRU_BLOCK_DIM)
        xcb = xc2.astype(BF16)
        r = jax.nn.sigmoid(_dot(xcb, wa_ref[d, 0]) + ba_ref[d:d + 1, :])
        gi = jax.nn.sigmoid(_dot(xcb, wx_ref[d, 0]) + bx_ref[d:d + 1, :])
        z = -lam_ref[d:d + 1, :]
        softplus = jnp.maximum(z, 0.0) + jnp.log1p(jnp.exp(-jnp.abs(z)))
        log_a = -LRU_C * r * softplus
        a = jnp.exp(log_a)
        one_m_a2 = -jnp.tanh(log_a) * (a * a + 1.0)
        b_in = jnp.sqrt(one_m_a2) * (gi * xc2)
        a_scr[d] = a.reshape(ct, nb, LRU_BLOCK_DIM)
        b_scr[d] = b_in.reshape(ct, nb, LRU_BLOCK_DIM)

    gates(xf_ref, xfp_ref, xfn_ref, j, 0)
    gates(xb_ref, xbp_ref, xbn_ref, nc - 1 - j, 1)

    def step(t, carry):
        hf, hb = carry
        tb = ct - 1 - t
        hf = a_scr[0, t] * hf + b_scr[0, t]
        hb = a_scr[1, tb] * hb + b_scr[1, tb]
        hf_ref[t] = hf
        hb_ref[tb] = hb
        return hf, hb

    hf, hb = lax.fori_loop(0, ct, step, (hf_state[...], hb_state[...]), unroll=8)
    hf_state[...] = hf
    hb_state[...] = hb


def _rglru(lrux_tm, conv_w, conv_b, lru_wa, lru_ba, lru_wx, lru_bx, lru_lambda, B, S, ct):
    nc = S // ct
    x3 = lrux_tm.reshape(S, B, LRU_WIDTH)
    main = lambda f: pl.BlockSpec((ct, B, LRU_BLOCK_DIM), lambda n, j: (f(j), 0, n))
    prev = lambda f: pl.BlockSpec((2, B, LRU_BLOCK_DIM),
                                  lambda n, j: (jnp.maximum(f(j) * (ct // 2) - 1, 0), 0, n))
    nxt = lambda f: pl.BlockSpec((1, B, LRU_BLOCK_DIM),
                                 lambda n, j: (jnp.minimum((f(j) + 1) * ct, S - 1), 0, n))
    fw = lambda j: j
    bw = lambda j: nc - 1 - j
    vec = lambda r: pl.BlockSpec((r, LRU_BLOCK_DIM), lambda n, j: (0, n))
    wspec = pl.BlockSpec((2, 1, LRU_BLOCK_DIM, LRU_BLOCK_DIM), lambda n, j: (0, n, 0, 0))
    hf, hb = pl.pallas_call(
        functools.partial(_rglru_kernel, ct=ct, nc=nc),
        grid=(LRU_BLOCKS, nc),
        in_specs=[main(fw), prev(fw), nxt(fw), main(bw), prev(bw), nxt(bw),
                  vec(CONV_WIDTH), vec(1), wspec, vec(2), wspec, vec(2), vec(2)],
        out_specs=[main(fw), main(bw)],
        out_shape=[jax.ShapeDtypeStruct((S, B, LRU_WIDTH), F32)] * 2,
        scratch_shapes=[
            pltpu.VMEM((ct + 3, B, LRU_BLOCK_DIM), F32),
            pltpu.VMEM((2, ct, B, LRU_BLOCK_DIM), F32),
            pltpu.VMEM((2, ct, B, LRU_BLOCK_DIM), F32),
            pltpu.VMEM((B, LRU_BLOCK_DIM), F32),
            pltpu.VMEM((B, LRU_BLOCK_DIM), F32),
        ],
        compiler_params=_cparams(("parallel", "arbitrary")),
        name="rglru",
    )(x3, x3, x3, x3, x3, x3, conv_w, conv_b[None, :], lru_wa.astype(BF16), lru_ba,
      lru_wx.astype(BF16), lru_bx, lru_lambda)
    return hf.reshape(S, B * LRU_WIDTH), hb.reshape(S, B * LRU_WIDTH)


def _mix_out_kernel(ya_ref, hf_ref, hb_ref, glu_ref, xn_ref, ga_ref, gl_ref, wo_ref,
                    g_ref, b_ref, wmq_ref, x1_ref, qm_ref):
    ya = _rms_norm(ya_ref[...], ga_ref[...])
    yl = _rms_norm((hf_ref[...] + hb_ref[...]) * glu_ref[...], gl_ref[...])
    nw = ya.shape[1]
    y = _dot(ya.astype(BF16), wo_ref[0:nw, :]) + _dot(yl.astype(BF16), wo_ref[nw:, :])
    x1 = _layer_norm(DN_ALPHA * xn_ref[...] + y, g_ref[...], b_ref[...])
    x1_ref[...] = x1
    qm_ref[...] = _dot(x1.astype(BF16), wmq_ref[...]).astype(BF16)


def _mix_out(y_attn, hf, hb, glu, xn, attn_out_g, lru_out_g, w_out, ln1_g, ln1_b, w_mq, B, S, tm):
    T, D = xn.shape
    nt = S // tm
    mw = w_mq.shape[1]
    tok = lambda w: pl.BlockSpec((tm, w), lambda b, i: (b * nt + i, 0))
    tmaj = pl.BlockSpec((tm, LRU_WIDTH), lambda b, i: (i, b))
    return pl.pallas_call(
        _mix_out_kernel,
        grid=(B, nt),
        in_specs=[tok(y_attn.shape[1]), tmaj, tmaj, tok(LRU_WIDTH), tok(D),
                  _const_spec((1, y_attn.shape[1])), _const_spec((1, LRU_WIDTH)),
                  _const_spec(w_out.shape), _const_spec((1, D)), _const_spec((1, D)),
                  _const_spec(w_mq.shape)],
        out_specs=[tok(D), tok(mw)],
        out_shape=[jax.ShapeDtypeStruct((T, D), F32), jax.ShapeDtypeStruct((T, mw), BF16)],
        compiler_params=_cparams(("parallel", "parallel")),
        name="mix_out",
    )(y_attn, hf, hb, glu, xn, attn_out_g[None, :], lru_out_g[None, :], w_out,
      ln1_g[None, :], ln1_b[None, :], w_mq)


def _mem_kv_kernel(mem_ref, wk_ref, wv_ref, k_ref, v_ref):
    m = mem_ref[...].astype(BF16)
    k_ref[...] = _dot(m, wk_ref[...]).astype(BF16)
    v_ref[...] = _dot(m, wv_ref[...]).astype(BF16)


def _mem_kv(mem2, w_mk, w_mv, B, M):
    D = mem2.shape[1]
    mw = w_mk.shape[1]
    blk = pl.BlockSpec((M, mw), lambda b: (b, 0))
    return pl.pallas_call(
        _mem_kv_kernel,
        grid=(B,),
        in_specs=[pl.BlockSpec((M, D), lambda b: (b, 0)), _const_spec(w_mk.shape), _const_spec(w_mv.shape)],
        out_specs=[blk, blk],
        out_shape=[jax.ShapeDtypeStruct((B * M, mw), BF16)] * 2,
        compiler_params=_cparams(("parallel",)),
        name="mem_kv",
    )(mem2, w_mk, w_mv)


def _xattn_pq_kernel(x1_ref, qm_ref, km_ref, vm_ref, wmo_ref, g_ref, b_ref, wpq_ref,
                     k1_ref, k2_ref, x2_ref, s1_ref, s2_ref):
    scale = MEM_HEAD_DIM ** -0.5
    outs = []
    for hd in range(MEM_HEADS):
        sl = slice(hd * MEM_HEAD_DIM, (hd + 1) * MEM_HEAD_DIM)
        s = _dot_nt(qm_ref[:, sl], km_ref[:, sl]) * scale
        m = jnp.max(s, axis=-1, keepdims=True)
        p = jnp.exp(s - m)
        l = jnp.sum(p, axis=-1, keepdims=True)
        outs.append((_dot(p.astype(BF16), vm_ref[:, sl]) / l).astype(BF16))
    o = jnp.concatenate(outs, axis=-1)
    y = _dot(o, wmo_ref[...])
    x2 = _layer_norm(DN_ALPHA * x1_ref[...] + y, g_ref[...], b_ref[...])
    x2_ref[...] = x2
    pq = _dot(x2.astype(BF16), wpq_ref[...])
    for hd in range(PEER_HEADS):
        base = hd * 2 * PEER_HALF
        q1 = pq[:, base:base + PEER_HALF].astype(BF16)
        q2 = pq[:, base + PEER_HALF:base + 2 * PEER_HALF].astype(BF16)
        s1_ref[hd] = _dot_nt(k1_ref[...], q1)
        s2_ref[hd] = _dot_nt(k2_ref[...], q2)


def _xattn_pq(x1, qm, kmem, vmem, w_mo, ln2_g, ln2_b, w_pq, keys1, keys2, B, S, M, tm):
    T, D = x1.shape
    nt = S // tm
    mw = qm.shape[1]
    tok = lambda w: pl.BlockSpec((tm, w), lambda b, i: (b * nt + i, 0))
    memb = pl.BlockSpec((M, mw), lambda b, i: (b, 0))
    sT = pl.BlockSpec((PEER_HEADS, N_KEYS, tm), lambda b, i: (0, 0, b * nt + i))
    return pl.pallas_call(
        _xattn_pq_kernel,
        grid=(B, nt),
        in_specs=[tok(D), tok(mw), memb, memb, _const_spec(w_mo.shape), _const_spec((1, D)),
                  _const_spec((1, D)), _const_spec(w_pq.shape), _const_spec(keys1.shape),
                  _const_spec(keys2.shape)],
        out_specs=[tok(D), sT, sT],
        out_shape=[jax.ShapeDtypeStruct((T, D), F32),
                   jax.ShapeDtypeStruct((PEER_HEADS, N_KEYS, T), F32),
                   jax.ShapeDtypeStruct((PEER_HEADS, N_KEYS, T), F32)],
        compiler_params=_cparams(("parallel", "parallel")),
        name="xattn_pq",
    )(x1, qm, kmem, vmem, w_mo, ln2_g[None, :], ln2_b[None, :], w_pq, keys1, keys2)


def _oddeven_merge_sort_pairs(n):
    pairs = []
    p = 1
    while p < n:
        k = p
        while k >= 1:
            for j in range(k % p, n - k, 2 * k):
                for i in range(min(k, n - j - k)):
                    if (i + j) // (2 * p) == (i + j + k) // (2 * p):
                        pairs.append((i + j, i + j + k))
            k //= 2
        p *= 2
    return pairs


_SORT16 = _oddeven_merge_sort_pairs(PEER_TOPK)


def _sort_desc(vals):
    vals = list(vals)
    for i, j in _SORT16:
        hi = jnp.maximum(vals[i], vals[j])
        lo = jnp.minimum(vals[i], vals[j])
        vals[i], vals[j] = hi, lo
    return vals


def _merge_top(a, b):
    n = PEER_TOPK
    c = []
    for i in range(n):
        other = b[n - 1 - i]
        c.append(a[i] if other is None else jnp.maximum(a[i], other))
    k = n // 2
    while k >= 1:
        for i in range(n):
            if (i // k) % 2 == 0:
                hi = jnp.maximum(c[i], c[i + k])
                lo = jnp.minimum(c[i], c[i + k])
                c[i], c[i + k] = hi, lo
        k //= 2
    return c


def _top16_of_128(ref, hd):
    best = None
    for g in range(N_KEYS // PEER_TOPK):
        grp = _sort_desc([ref[hd, g * PEER_TOPK + i] for i in range(PEER_TOPK)])
        best = grp if best is None else _merge_top(best, grp)
    return best


def _peer_topk_kernel(s1_ref, s2_ref, tau_ref, m1_ref, m2_ref, zinv_ref):
    def per_head(hd, carry):
        t1 = _top16_of_128(s1_ref, hd)
        t2 = _top16_of_128(s2_ref, hd)
        rows = [[t1[i] + t2[j] for j in range(PEER_TOPK // (i + 1))] for i in range(PEER_TOPK)]
        g1 = _sort_desc(rows[1] + rows[2] + rows[4])
        g2 = _sort_desc(rows[3] + rows[5] + rows[6] + rows[7] + [rows[i][0] for i in range(8, 14)])
        top = _merge_top(_merge_top(rows[0], g1), g2)
        tail = [rows[14][0], rows[15][0]] + [None] * (PEER_TOPK - 2)
        top = _merge_top(top, tail)
        z = jnp.ones_like(top[0])
        for kk in range(1, PEER_TOPK):
            z = z + jnp.exp(top[kk] - top[0])
        tau_ref[hd] = top[PEER_TOPK - 1]
        m1_ref[hd] = t1[0]
        m2_ref[hd] = t2[0]
        zinv_ref[hd] = 1.0 / z
        return carry

    lax.fori_loop(0, PEER_HEADS, per_head, 0)


def _peer_topk(s1T, s2T, T):
    rows = SUBLANES
    nblk = T // (rows * LANES)
    s1v = s1T.reshape(PEER_HEADS, N_KEYS, T // LANES, LANES)
    s2v = s2T.reshape(PEER_HEADS, N_KEYS, T // LANES, LANES)
    sspec = pl.BlockSpec((PEER_HEADS, N_KEYS, rows, LANES), lambda i: (0, 0, i, 0))
    ospec = pl.BlockSpec((PEER_HEADS, rows, LANES), lambda i: (0, i, 0))
    oshape = jax.ShapeDtypeStruct((PEER_HEADS, T // LANES, LANES), F32)
    outs = pl.pallas_call(
        _peer_topk_kernel,
        grid=(nblk,),
        in_specs=[sspec, sspec],
        out_specs=[ospec] * 4,
        out_shape=[oshape] * 4,
        compiler_params=_cparams(("parallel",)),
        name="peer_topk",
    )(s1v, s2v)
    return [o.reshape(PEER_HEADS, 1, T) for o in outs]


def _peer_ffn_kernel(x2_ref, s1_ref, s2_ref, tau_ref, m1_ref, m2_ref, zinv_ref, u_ref, v_ref,
                     g_ref, b_ref, out_ref,
                     xb_scr, e1_scr, e2_scr, act_scr, a_scr, acc_scr, *, eb, tm, n_eblk):
    j = pl.program_id(1)
    n_c = tm // LANES
    assert eb == SUBLANES * N_KEYS

    @pl.when(j == 0)
    def _():
        xb_scr[...] = x2_ref[...].astype(BF16)
        acc_scr[...] = jnp.zeros_like(acc_scr)
        for hd in range(PEER_HEADS):
            e1_scr[hd] = jnp.exp(s1_ref[hd] - m1_ref[hd][None]) * zinv_ref[hd][None]
            e2_scr[hd] = jnp.exp(s2_ref[hd] - m2_ref[hd])

    act_scr[...] = _dot_nt(u_ref[...], xb_scr[...])

    def chunk(c, carry):
        col = pl.multiple_of(c * LANES, LANES)
        for al in range(SUBLANES):
            w = jnp.zeros((N_KEYS, LANES), F32)
            for hd in range(PEER_HEADS):
                s1row = s1_ref[hd, j, al:al + 1, pl.ds(col, LANES)]
                e1row = e1_scr[hd, j, al:al + 1, pl.ds(col, LANES)]
                taurow = tau_ref[hd, :, pl.ds(col, LANES)]
                keep = (s2_ref[hd, :, pl.ds(col, LANES)] + s1row) >= taurow
                w = w + jnp.where(keep, e2_scr[hd, :, pl.ds(col, LANES)], 0.0) * e1row
            rows = slice(al * N_KEYS, (al + 1) * N_KEYS)
            act = act_scr[rows, pl.ds(col, LANES)]
            a_scr[rows, pl.ds(col, LANES)] = (_gelu_erf(act) * w).astype(BF16)
        return carry

    lax.fori_loop(0, n_c, chunk, 0)
    acc_scr[...] += _dot_tn(a_scr[...], v_ref[...])

    @pl.when(j == n_eblk - 1)
    def _():
        out_ref[...] = _layer_norm(DN_ALPHA * x2_ref[...] + acc_scr[...], g_ref[...], b_ref[...])


def _peer_ffn(x2, s1T, s2T, tau, m1, m2, zinv, u_b, v_b, ln3_g, ln3_b, tm, eb):
    T, D = x2.shape
    n_exp = u_b.shape[0]
    n_eblk = n_exp // eb
    one = pl.Buffered(1)
    n_grp = N_KEYS // SUBLANES
    s1g = s1T.reshape(PEER_HEADS, n_grp, SUBLANES, T)
    s1spec = pl.BlockSpec((PEER_HEADS, n_grp, SUBLANES, tm), lambda i, j: (0, 0, 0, i), pipeline_mode=one)
    s2spec = pl.BlockSpec((PEER_HEADS, N_KEYS, tm), lambda i, j: (0, 0, i), pipeline_mode=one)
    rspec = pl.BlockSpec((PEER_HEADS, 1, tm), lambda i, j: (0, 0, i), pipeline_mode=one)
    wspec = pl.BlockSpec((eb, D), lambda i, j: (j, 0))
    return pl.pallas_call(
        functools.partial(_peer_ffn_kernel, eb=eb, tm=tm, n_eblk=n_eblk),
        grid=(T // tm, n_eblk),
        in_specs=[pl.BlockSpec((tm, D), lambda i, j: (i, 0), pipeline_mode=one), s1spec, s2spec,
                  rspec, rspec, rspec, rspec, wspec, wspec, _const_spec((1, D)), _const_spec((1, D))],
        out_specs=pl.BlockSpec((tm, D), lambda i, j: (i, 0)),
        out_shape=jax.ShapeDtypeStruct((T, D), F32),
        scratch_shapes=[
            pltpu.VMEM((tm, D), BF16),
            pltpu.VMEM((PEER_HEADS, n_grp, SUBLANES, tm), F32),
            pltpu.VMEM((PEER_HEADS, N_KEYS, tm), F32),
            pltpu.VMEM((eb, tm), F32),
            pltpu.VMEM((eb, tm), BF16),
            pltpu.VMEM((tm, D), F32),
        ],
        compiler_params=_cparams(("parallel", "arbitrary")),
        name="peer_ffn",
    )(x2, s1g, s2T, tau, m1, m2, zinv, u_b, v_b, ln3_g[None, :], ln3_b[None, :])


def _swap_halves(w):
    return jnp.concatenate([w[..., QK_ROPE // 2:], w[..., :QK_ROPE // 2]], axis=-1)


def _prep_w_in(w_in):
    kpe = w_in[:, Q_LORA + KV_LORA:Q_LORA + KV_LORA + QK_ROPE]
    rest = w_in[:, Q_LORA + KV_LORA + QK_ROPE:]
    return jnp.concatenate([w_in[:, :Q_LORA + KV_LORA], kpe, _swap_halves(kpe), rest], axis=1).astype(BF16)


def _prep_w_uq(w_uq):
    w = w_uq.reshape(Q_LORA, MLA_HEADS, QK_NOPE + QK_ROPE)
    pe = w[..., QK_NOPE:]
    return jnp.concatenate([w[..., :QK_NOPE], pe, _swap_halves(pe)], axis=-1).reshape(
        Q_LORA, MLA_HEADS * HEAD_PAD).astype(BF16)


def _prep_w_ukv(w_ukv):
    w = w_ukv.reshape(KV_LORA, MLA_HEADS, QK_NOPE + V_HEAD)
    return jnp.concatenate([w[..., :QK_NOPE].reshape(KV_LORA, -1), w[..., QK_NOPE:].reshape(KV_LORA, -1)],
                           axis=1).astype(BF16)


def kernel(x, mem, positions, in_ln_g, in_ln_b, w_in, q_norm_g, w_uq, kv_norm_g, w_ukv, conv_w, conv_b, lru_wa, lru_ba, lru_wx, lru_bx, lru_lambda, attn_out_g, lru_out_g, w_out, ln1_g, ln1_b, w_mq, w_mk, w_mv, w_mo, ln2_g, ln2_b, w_pq, sub_keys1, sub_keys2, peer_u, peer_v, ln3_g, ln3_b):
    B, S, D = x.shape
    M = mem.shape[1]
    T = B * S
    assert w_in.shape[0] == DEPTH == 1
    assert B == SUBLANES, "rglru keeps the batch on the sublane axis"
    l = 0
    tm = min(256, S)

    xn, qcat, kcat, v, lrux_tm, glu = _in_proj(
        x.reshape(T, D), positions.reshape(T, 1), in_ln_g, in_ln_b, _prep_w_in(w_in[l]),
        q_norm_g[l], _prep_w_uq(w_uq[l]), kv_norm_g[l], _prep_w_ukv(w_ukv[l]), B, S, tm)
    y_attn = _mla_attn(qcat, kcat, v, B, S, min(512, S))
    hf, hb = _rglru(lrux_tm, conv_w[l], conv_b[l], lru_wa[l], lru_ba[l], lru_wx[l], lru_bx[l],
                    lru_lambda[l], B, S, min(256, S))
    x1, qm = _mix_out(y_attn, hf, hb, glu, xn, attn_out_g[l], lru_out_g[l], w_out[l].astype(BF16),
                      ln1_g[l], ln1_b[l], w_mq[l].astype(BF16), B, S, tm)
    kmem, vmem = _mem_kv(mem.reshape(B * M, D), w_mk[l].astype(BF16), w_mv[l].astype(BF16), B, M)
    x2, s1T, s2T = _xattn_pq(x1, qm, kmem, vmem, w_mo[l].astype(BF16), ln2_g[l], ln2_b[l],
                             w_pq[l].astype(BF16), sub_keys1[l].astype(BF16),
                             sub_keys2[l].astype(BF16), B, S, M, tm)
    tau, m1, m2, zinv = _peer_topk(s1T, s2T, T)
    out = _peer_ffn(x2, s1T, s2T, tau, m1, m2, zinv, peer_u[l].astype(BF16), peer_v[l].astype(BF16),
                    ln3_g[l], ln3_b[l], min(512, T), SUBLANES * N_KEYS)
    return out.reshape(B, S, D)
```
